```python
import math
import jax, jax.numpy as jnp
from jax import lax
import numpy as np

D_MODEL = 2048
BATCH = 2
SEQ = 4096
DEPTH = 1

EPS = 1e-6
GLA_HEADS = 8
GLA_DV = D_MODEL // 2 // GLA_HEADS
GLA_DK = GLA_DV // 2
GLA_GATE_RANK = 16
GLA_GATE_NORMALIZER = 16.0
GLA_CHUNK = 64
MOBA_HEADS = 8
MOBA_HEAD_DIM = D_MODEL // 2 // MOBA_HEADS
MOBA_BLOCK = 256
MOBA_TOPK = 3
MOBA_QCHUNK = 32
ROPE_THETA = 500000.0
ROPE_DIMS = MOBA_HEAD_DIM // 4
PEER_HEADS = 8
PEER_N_KEYS = 128
PEER_N_EXPERTS = PEER_N_KEYS * PEER_N_KEYS
PEER_QUERY_DIM = 256
PEER_TOPK = 16
PEER_TOKEN_CHUNK = 128
IN_SPLITS = (GLA_HEADS * GLA_DK, GLA_HEADS * GLA_DK, GLA_HEADS * GLA_DV, GLA_HEADS * GLA_DV, GLA_GATE_RANK,
             MOBA_HEADS * MOBA_HEAD_DIM, MOBA_HEADS * MOBA_HEAD_DIM, MOBA_HEADS * MOBA_HEAD_DIM)
IN_WIDTH = sum(IN_SPLITS)
MIX_WIDTH = GLA_HEADS * GLA_DV + MOBA_HEADS * MOBA_HEAD_DIM

kernel_name = "hybrid_gla_moba_peer_adaln"


def rms_norm(x, g):
    xf = x.astype(jnp.float32)
    y = xf * lax.rsqrt(jnp.mean(xf * xf, axis=-1, keepdims=True) + EPS)
    return (y * g.astype(jnp.float32)).astype(x.dtype)


def modulate(h, shift, scale):
    return h * (1.0 + scale[:, None, :]) + shift[:, None, :]


def partial_rotary(x, positions):
    half = ROPE_DIMS // 2
    inv_freq = jnp.power(jnp.float32(ROPE_THETA), -jnp.arange(half, dtype=jnp.float32) * (2.0 / ROPE_DIMS))
    ang = positions.astype(jnp.float32)[:, None, :, None] * inv_freq
    cos, sin = jnp.cos(ang), jnp.sin(ang)
    xr = x[..., :ROPE_DIMS].astype(jnp.float32)
    x1, x2 = xr[..., :half], xr[..., half:]
    rot = jnp.concatenate([x1 * cos - x2 * sin, x2 * cos + x1 * sin], axis=-1)
    return jnp.concatenate([rot.astype(x.dtype), x[..., ROPE_DIMS:]], axis=-1)


def gla_attention(q, k, v, log_a):
    B, H, S, DK = q.shape
    DV = v.shape[-1]
    L = GLA_CHUNK
    n = S // L
    f32 = jnp.float32
    rs = lambda t: t.astype(f32).reshape(B, H, n, L, t.shape[-1])
    q = rs(q) * (DK ** -0.5)
    k, v, log_a = rs(k), rs(v), rs(log_a)
    b = jnp.cumsum(log_a, axis=3)
    q_dec = q * jnp.exp(b)
    k_dec = k * jnp.exp(-b)
    causal = jnp.tril(jnp.ones((L, L), dtype=bool))
    attn = jnp.where(causal, jnp.einsum('bhnld,bhnmd->bhnlm', q_dec, k_dec), 0.0)
    o_intra = jnp.einsum('bhnlm,bhnmv->bhnlv', attn, v)
    b_last = b[:, :, :, -1:, :]
    chunk_update = jnp.einsum('bhnld,bhnlv->bhndv', k * jnp.exp(b_last - b), v)
    chunk_decay = jnp.exp(b_last[:, :, :, 0, :])

    def step(state, inp):
        dec, upd = inp
        return dec[..., None] * state + upd, state

    _, states = lax.scan(step, jnp.zeros((B, H, DK, DV), f32),
                         (jnp.moveaxis(chunk_decay, 2, 0), jnp.moveaxis(chunk_update, 2, 0)))
    states = jnp.moveaxis(states, 0, 2)
    o_inter = jnp.einsum('bhnld,bhndv->bhnlv', q_dec, states)
    return (o_intra + o_inter).reshape(B, H, S, DV)


def moba_attention(q, k, v):
    B, H, S, Dh = q.shape
    P = MOBA_BLOCK
    S_pad = -(-S // P) * P
    pad = S_pad - S
    if pad:
        padw = ((0, 0), (0, 0), (0, pad), (0, 0))
        q, k, v = jnp.pad(q, padw), jnp.pad(k, padw), jnp.pad(v, padw)
    nb = S_pad // P
    kb = k.reshape(B, H, nb, P, Dh)
    vb = v.reshape(B, H, nb, P, Dh)
    k_mean = jnp.mean(kb.astype(jnp.float32), axis=3)
    gate = jnp.einsum('bhsd,bhnd->bhsn', q.astype(jnp.float32), k_mean)
    q_block = jnp.arange(S_pad) // P
    past = jnp.arange(nb)[None, :] < q_block[:, None]
    gate = jnp.where(past, gate, -jnp.inf)
    n_sel = min(MOBA_TOPK, nb)
    _, sel = lax.top_k(gate, n_sel)
    sel_valid = sel < q_block[:, None]
    scale = Dh ** -0.5
    QC = MOBA_QCHUNK
    nq = S_pad // QC
    to_chunks = lambda t: jnp.moveaxis(t.reshape(B, H, nq, QC, t.shape[-1]), 2, 0)
    b_idx = jnp.arange(B)[:, None, None, None]
    h_idx = jnp.arange(H)[None, :, None, None]

    def chunk(args):
        ci, q_c, sel_c, valid_c = args
        blk = (ci * QC) // P
        k_sel = kb[b_idx, h_idx, sel_c]
        v_sel = vb[b_idx, h_idx, sel_c]
        k_own = lax.dynamic_index_in_dim(kb, blk, axis=2, keepdims=False)
        v_own = lax.dynamic_index_in_dim(vb, blk, axis=2, keepdims=False)
        s_sel = jnp.einsum('bhqd,bhqnpd->bhqnp', q_c, k_sel, preferred_element_type=jnp.float32) * scale
        s_sel = jnp.where(valid_c[..., None], s_sel, -jnp.inf).reshape(B, H, QC, n_sel * P)
        s_own = jnp.einsum('bhqd,bhpd->bhqp', q_c, k_own, preferred_element_type=jnp.float32) * scale
        q_pos = ci * QC + jnp.arange(QC)
        k_pos = blk * P + jnp.arange(P)
        s_own = jnp.where(k_pos[None, :] <= q_pos[:, None], s_own, -jnp.inf)
        p = jax.nn.softmax(jnp.concatenate([s_sel, s_own], axis=-1), axis=-1)
        p_sel = p[..., :n_sel * P].reshape(B, H, QC, n_sel, P).astype(v.dtype)
        p_own = p[..., n_sel * P:].astype(v.dtype)
        return (jnp.einsum('bhqnp,bhqnpd->bhqd', p_sel, v_sel)
                + jnp.einsum('bhqp,bhpd->bhqd', p_own, v_own))

    out = lax.map(chunk, (jnp.arange(nq), to_chunks(q), to_chunks(sel), to_chunks(sel_valid)))
    out = jnp.moveaxis(out, 0, 2).reshape(B, H, S_pad, Dh)
    return out[:, :, :S]


def peer_ffn(h, w_q, sub_keys, u, v):
    B, S, D = h.shape
    T = B * S
    K = PEER_TOPK
    hf = h.reshape(T, D)
    q = (hf @ w_q).reshape(T, PEER_HEADS, 2, PEER_QUERY_DIM // 2)
    s = jnp.einsum('thcd,hckd->thck', q, sub_keys, preferred_element_type=jnp.float32)
    s1, i1 = lax.top_k(s[:, :, 0], K)
    s2, i2 = lax.top_k(s[:, :, 1], K)
    cand_s = (s1[..., :, None] + s2[..., None, :]).reshape(T, PEER_HEADS, K * K)
    cand_i = (i1[..., :, None] * PEER_N_KEYS + i2[..., None, :]).reshape(T, PEER_HEADS, K * K)
    top_s, pos = lax.top_k(cand_s, K)
    expert = jnp.take_along_axis(cand_i, pos, axis=-1)
    gates = jax.nn.softmax(top_s, axis=-1)
    TC = PEER_TOKEN_CHUNK
    nc = T // TC

    def chunk(args):
        h_c, e_c, g_c = args
        act = jax.nn.gelu(jnp.einsum('td,thkd->thk', h_c, u[e_c], preferred_element_type=jnp.float32),
                          approximate=False)
        w = (g_c * act).astype(h.dtype)
        return jnp.einsum('thk,thkd->td', w, v[e_c])

    out = lax.map(chunk, (hf.reshape(nc, TC, D), expert.reshape(nc, TC, PEER_HEADS, K),
                          gates.reshape(nc, TC, PEER_HEADS, K)))
    return out.reshape(B, S, D)


def setup_inputs(seed: int = 0) -> dict:
    key = jax.random.key(seed)
    ks = jax.random.split(key, 16)
    D = D_MODEL
    nrm = lambda k, shape, s: jax.random.normal(k, shape, jnp.float32) * s
    return {
        "x": nrm(ks[0], (BATCH, SEQ, D), 1.0),
        "c": nrm(ks[1], (BATCH, D), 1.0),
        "positions": jnp.broadcast_to(jnp.arange(SEQ, dtype=jnp.int32)[None, :], (BATCH, SEQ)),
        "w_ada": nrm(ks[2], (D, 6 * D), 0.5 * D ** -0.5),
        "b_ada": nrm(ks[3], (6 * D,), 0.01),
        "norm1_g": 1.0 + nrm(ks[4], (D,), 0.02),
        "w_in": nrm(ks[5], (D, IN_WIDTH), D ** -0.5),
        "gla_w_a2": nrm(ks[6], (GLA_GATE_RANK, GLA_HEADS * GLA_DK), GLA_GATE_RANK ** -0.5),
        "gla_b_a": nrm(ks[7], (GLA_HEADS * GLA_DK,), 0.1),
        "gla_norm_g": 1.0 + nrm(ks[8], (GLA_DV,), 0.02),
        "moba_q_norm_g": 1.0 + nrm(ks[9], (MOBA_HEAD_DIM,), 0.02),
        "moba_k_norm_g": 1.0 + nrm(ks[10], (MOBA_HEAD_DIM,), 0.02),
        "w_out": nrm(ks[11], (MIX_WIDTH, D), MIX_WIDTH ** -0.5),
        "norm2_g": 1.0 + nrm(ks[12], (D,), 0.02),
        "peer_w_q": nrm(ks[13], (D, PEER_HEADS * PEER_QUERY_DIM), D ** -0.5),
        "peer_sub_keys": nrm(ks[14], (PEER_HEADS, 2, PEER_N_KEYS, PEER_QUERY_DIM // 2), (PEER_QUERY_DIM // 2) ** -0.5),
        "peer_u": nrm(jax.random.fold_in(ks[15], 0), (PEER_N_EXPERTS, D), D ** -0.5),
        "peer_v": nrm(jax.random.fold_in(ks[15], 1), (PEER_N_EXPERTS, D), 1.0),
    }


def reference(x, c, positions, w_ada, b_ada, norm1_g, w_in, gla_w_a2, gla_b_a, gla_norm_g,
              moba_q_norm_g, moba_k_norm_g, w_out, norm2_g, peer_w_q, peer_sub_keys, peer_u, peer_v):
    B, S, D = x.shape
    for _ in range(DEPTH):
        mod = jax.nn.silu(c) @ w_ada + b_ada
        shift1, scale1, gate1, shift2, scale2, gate2 = jnp.split(mod, 6, axis=-1)

        h = modulate(rms_norm(x, norm1_g), shift1, scale1)
        proj = h @ w_in
        offs = np.cumsum(IN_SPLITS)[:-1].tolist()
        gq, gk, gv, gg, ga, mq, mk, mv = jnp.split(proj, offs, axis=-1)

        heads = lambda t, nh: jnp.transpose(t.reshape(B, S, nh, -1), (0, 2, 1, 3))
        log_a = jax.nn.log_sigmoid((ga @ gla_w_a2 + gla_b_a).astype(jnp.float32)) / GLA_GATE_NORMALIZER
        o_gla = gla_attention(heads(gq, GLA_HEADS), heads(gk, GLA_HEADS), heads(gv, GLA_HEADS),
                              heads(log_a, GLA_HEADS)).astype(x.dtype)
        o_gla = rms_norm(jnp.transpose(o_gla, (0, 2, 1, 3)), gla_norm_g)
        o_gla = (o_gla * jax.nn.silu(gg.reshape(B, S, GLA_HEADS, GLA_DV))).reshape(B, S, GLA_HEADS * GLA_DV)

        qm = rms_norm(mq.reshape(B, S, MOBA_HEADS, MOBA_HEAD_DIM), moba_q_norm_g)
        km = rms_norm(mk.reshape(B, S, MOBA_HEADS, MOBA_HEAD_DIM), moba_k_norm_g)
        qm = partial_rotary(jnp.transpose(qm, (0, 2, 1, 3)), positions)
        km = partial_rotary(jnp.transpose(km, (0, 2, 1, 3)), positions)
        o_moba = moba_attention(qm, km, heads(mv, MOBA_HEADS))
        o_moba = jnp.transpose(o_moba, (0, 2, 1, 3)).reshape(B, S, MOBA_HEADS * MOBA_HEAD_DIM)

        mix = jnp.concatenate([o_gla, o_moba], axis=-1) @ w_out
        x = x + gate1[:, None, :] * mix

        h2 = modulate(rms_norm(x, norm2_g), shift2, scale2)
        x = x + gate2[:, None, :] * peer_ffn(h2, peer_w_q, peer_sub_keys, peer_u, peer_v)
    return x
```

```python
import functools
import math

import jax
import jax.numpy as jnp
from jax import lax
from jax.experimental import pallas as pl
from jax.experimental.pallas import tpu as pltpu

F32 = jnp.float32
BF16 = jnp.bfloat16

EPS = 1e-6
GLA_HEADS = 8
GLA_DK = 64
GLA_DV = 128
GLA_GATE_RANK = 16
GLA_GATE_NORMALIZER = 16.0
GLA_CHUNK = 64
MOBA_HEADS = 8
MOBA_HEAD_DIM = 128
MOBA_BLOCK = 256
MOBA_TOPK = 3
ROPE_THETA = 500000.0
ROPE_DIMS = 32
PEER_HEADS = 8
PEER_N_KEYS = 128
PEER_TOPK = 16

LANES = 128
VMEM_LIMIT = 48 * 1024 * 1024

NEG_INF = float("-inf")


def _dot(a, b):
    return jnp.dot(a, b, preferred_element_type=F32)


def _dot_nt(a, b):
    return lax.dot_general(a, b, (((1,), (1,)), ((), ())), preferred_element_type=F32)


def _dot_tn(a, b):
    return lax.dot_general(a, b, (((0,), (0,)), ((), ())), preferred_element_type=F32)


def _params(sem):
    return pltpu.CompilerParams(dimension_semantics=sem, vmem_limit_bytes=VMEM_LIMIT)


def _ada_kernel(ct_ref, w_ref, b_ref, o_ref):
    ct = ct_ref[...]
    sc = ct * jax.nn.sigmoid(ct)
    w = w_ref[...]
    for b in range(ct.shape[1]):
        o_ref[b:b + 1, :] = jnp.sum(w * sc[:, b:b + 1], axis=0, keepdims=True) + b_ref[...]


def _ada(c, w_ada, b_ada):
    B, D = c.shape
    N = w_ada.shape[1]
    tn = 1024
    return pl.pallas_call(
        _ada_kernel,
        out_shape=jax.ShapeDtypeStruct((B, N), F32),
        grid=(N // tn,),
        in_specs=[pl.BlockSpec((D, B), lambda j: (0, 0)),
                  pl.BlockSpec((D, tn), lambda j: (0, j)),
                  pl.BlockSpec((1, tn), lambda j: (0, j))],
        out_specs=pl.BlockSpec((B, tn), lambda j: (0, j)),
        compiler_params=_params(("arbitrary",)),
        name="ada",
    )(c.T, w_ada, b_ada.reshape(1, N))


def _rms_mod(x, g, shift, scale):
    var = jnp.mean(x * x, axis=-1, keepdims=True)
    y = x * lax.rsqrt(var + EPS) * g
    return y * (1.0 + scale) + shift


def _inproj_kernel(x_ref, shift_ref, scale_ref, g_ref, w_ref, wga_ref, o_ref, ga_ref, h_scr):
    @pl.when(pl.program_id(2) == 0)
    def _():
        hb = _rms_mod(x_ref[0], g_ref[...], shift_ref[0], scale_ref[0]).astype(BF16)
        h_scr[...] = hb
        ga_ref[0] = _dot(hb, wga_ref[...])

    o_ref[0] = _dot(h_scr[...], w_ref[...])


def _inproj(x, mod3, norm_g, w_main, w_ga):
    B, S, D = x.shape
    N = w_main.shape[1]
    tm, tn = 512, 1024
    return pl.pallas_call(
        _inproj_kernel,
        out_shape=(jax.ShapeDtypeStruct((B, S, N), F32),
                   jax.ShapeDtypeStruct((B, S, LANES), F32)),
        grid=(B, S // tm, N // tn),
        in_specs=[pl.BlockSpec((1, tm, D), lambda b, i, j: (b, i, 0)),
                  pl.BlockSpec((1, 1, D), lambda b, i, j: (b, 0, 0)),
                  pl.BlockSpec((1, 1, D), lambda b, i, j: (b, 0, 1)),
                  pl.BlockSpec((1, D), lambda b, i, j: (0, 0)),
                  pl.BlockSpec((D, tn), lambda b, i, j: (0, j)),
                  pl.BlockSpec((D, LANES), lambda b, i, j: (0, 0))],
        out_specs=(pl.BlockSpec((1, tm, tn), lambda b, i, j: (b, i, j)),
                   pl.BlockSpec((1, tm, LANES), lambda b, i, j: (b, i, 0))),
        scratch_shapes=[pltpu.VMEM((tm, D), BF16)],
        compiler_params=_params(("parallel", "parallel", "arbitrary")),
        name="inproj",
    )(x, mod3, mod3, norm_g.reshape(1, D), w_main, w_ga)


def _log_sigmoid(z):
    return jnp.minimum(z, 0.0) - jnp.log1p(jnp.exp(-jnp.abs(z)))


def _gla_kernel(q_ref, k_ref, v_ref, gg_ref, ga_ref, wa_ref, ba_ref, gn_ref, o_ref, st_scr, *, ts):
    L = GLA_CHUNK

    @pl.when(pl.program_id(2) == 0)
    def _():
        st_scr[...] = jnp.zeros_like(st_scr)

    z = _dot(ga_ref[0].astype(BF16), wa_ref[...]) + ba_ref[...]
    la = _log_sigmoid(z) * (1.0 / GLA_GATE_NORMALIZER)

    row = lax.broadcasted_iota(jnp.int32, (L, L), 0)
    col = lax.broadcasted_iota(jnp.int32, (L, L), 1)
    causal = row >= col
    tri = jnp.where(causal, 1.0, 0.0).astype(BF16)
    lane = lax.broadcasted_iota(jnp.int32, (1, LANES), 1)
    head_mask = [lane < GLA_DK, lane >= GLA_DK]
    gn = gn_ref[...]

    states = [st_scr[0], st_scr[1]]
    for c in range(ts // L):
        r0 = c * L
        la_c = la[r0:r0 + L]
        hi = la_c.astype(BF16)
        lo = (la_c - hi.astype(F32)).astype(BF16)
        b = _dot(tri, hi) + _dot(tri, lo)
        b_last = b[L - 1:L, :]
        q_c = q_ref[0, r0:r0 + L, :]
        k_c = k_ref[0, r0:r0 + L, :]
        qd = q_c * (GLA_DK ** -0.5) * jnp.exp(b)
        kd = (k_c * jnp.exp(-b)).astype(BF16)
        ku = (k_c * jnp.exp(b_last - b)).astype(BF16)
        dec = jnp.exp(b_last)
        for h in range(2):
            qh = jnp.where(head_mask[h], qd, 0.0).astype(BF16)
            attn = jnp.where(causal, _dot_nt(qh, kd), 0.0)
            v_h = v_ref[0, r0:r0 + L, h * GLA_DV:(h + 1) * GLA_DV].astype(BF16)
            o = _dot(attn.astype(BF16), v_h) + _dot_nt(qh, states[h].astype(BF16))
            states[h] = states[h] * dec + _dot_tn(v_h, ku)
            var = jnp.mean(o * o, axis=-1, keepdims=True)
            on = o * lax.rsqrt(var + EPS) * gn
            g = gg_ref[0, r0:r0 + L, h * GLA_DV:(h + 1) * GLA_DV]
            o_ref[0, r0:r0 + L, h * GLA_DV:(h + 1) * GLA_DV] = (on * (g * jax.nn.sigmoid(g))).astype(BF16)
    st_scr[0] = states[0]
    st_scr[1] = states[1]


def _gla(proj, ga, w_a2p, b_a, gn):
    B, S, _ = proj.shape
    ts = 512
    hp = GLA_HEADS // 2
    kern = functools.partial(_gla_kernel, ts=ts)
    return pl.pallas_call(
        kern,
        out_shape=jax.ShapeDtypeStruct((B, S, GLA_HEADS * GLA_DV), BF16),
        grid=(B, hp, S // ts),
        in_specs=[pl.BlockSpec((1, ts, LANES), lambda b, p, s: (b, s, p)),
                  pl.BlockSpec((1, ts, LANES), lambda b, p, s: (b, s, 4 + p)),
                  pl.BlockSpec((1, ts, 2 * GLA_DV), lambda b, p, s: (b, s, 4 + p)),
                  pl.BlockSpec((1, ts, 2 * GLA_DV), lambda b, p, s: (b, s, 8 + p)),
                  pl.BlockSpec((1, ts, LANES), lambda b, p, s: (b, s, 0)),
                  pl.BlockSpec((LANES, LANES), lambda b, p, s: (0, p)),
                  pl.BlockSpec((1, LANES), lambda b, p, s: (0, p)),
                  pl.BlockSpec((1, GLA_DV), lambda b, p, s: (0, 0))],
        out_specs=pl.BlockSpec((1, ts, 2 * GLA_DV), lambda b, p, s: (b, s, p)),
        scratch_shapes=[pltpu.VMEM((2, GLA_DV, LANES), F32)],
        compiler_params=_params(("parallel", "parallel", "arbitrary")),
        name="gla",
    )(proj, proj, proj, proj, ga, w_a2p, b_a.reshape(1, -1), gn.reshape(1, -1))


def _moba_prep_kernel(q_ref, k_ref, v_ref, pos_ref, invf_ref, gq_ref, gk_ref,
                      qo_ref, ko_ref, vo_ref, km_ref):
    ang = pos_ref[0] * invf_ref[...]
    cs = jnp.cos(ang)
    sn = jnp.sin(ang)
    lane = lax.broadcasted_iota(jnp.int32, (1, LANES), 1)
    half = ROPE_DIMS // 2
    s_lo = jnp.where(lane < half, -sn, 0.0)
    s_hi = jnp.where(lane < half, 0.0, jnp.where(lane < ROPE_DIMS, sn, 0.0))

    def norm_rot(x, g):
        var = jnp.mean(x * x, axis=-1, keepdims=True)
        y = x * lax.rsqrt(var + EPS) * g
        return (y * cs + pltpu.roll(y, LANES - half, 1) * s_lo + pltpu.roll(y, half, 1) * s_hi)

    for h in range(MOBA_HEADS):
        sl = slice(h * MOBA_HEAD_DIM, (h + 1) * MOBA_HEAD_DIM)
        qo_ref[0, :, sl] = norm_rot(q_ref[0, :, sl], gq_ref[...]).astype(BF16)
        kr = norm_rot(k_ref[0, :, sl], gk_ref[...])
        ko_ref[0, :, sl] = kr.astype(BF16)
        km_ref[0, 0, :, sl] = jnp.mean(kr, axis=0, keepdims=True)
    vo_ref[0] = v_ref[0].astype(BF16)


def _moba_prep(proj, pos3, invf, gq, gk):
    B, S, _ = proj.shape
    P = MOBA_BLOCK
    W = MOBA_HEADS * MOBA_HEAD_DIM
    nb = S // P
    return pl.pallas_call(
        _moba_prep_kernel,
        out_shape=(jax.ShapeDtypeStruct((B, S, W), BF16),
                   jax.ShapeDtypeStruct((B, S, W), BF16),
                   jax.ShapeDtypeStruct((B, S, W), BF16),
                   jax.ShapeDtypeStruct((B, nb, 1, W), F32)),
        grid=(B, nb),
        in_specs=[pl.BlockSpec((1, P, W), lambda b, n: (b, n, 3)),
                  pl.BlockSpec((1, P, W), lambda b, n: (b, n, 4)),
                  pl.BlockSpec((1, P, W), lambda b, n: (b, n, 5)),
                  pl.BlockSpec((1, P, 1), lambda b, n: (b, n, 0)),
                  pl.BlockSpec((1, LANES), lambda b, n: (0, 0)),
                  pl.BlockSpec((1, MOBA_HEAD_DIM), lambda b, n: (0, 0)),
                  pl.BlockSpec((1, MOBA_HEAD_DIM), lambda b, n: (0, 0))],
        out_specs=(pl.BlockSpec((1, P, W), lambda b, n: (b, n, 0)),
                   pl.BlockSpec((1, P, W), lambda b, n: (b, n, 0)),
                   pl.BlockSpec((1, P, W), lambda b, n: (b, n, 0)),
                   pl.BlockSpec((1, 1, 1, W), lambda b, n: (b, n, 0, 0))),
        compiler_params=_params(("parallel", "parallel")),
        name="moba_prep",
    )(proj, proj, proj, pos3, invf, gq.reshape(1, -1), gk.reshape(1, -1))


def _moba_kernel(q_ref, k_ref, v_ref, km_ref, o_ref, m_scr, l_scr, acc_scr, sel_scr, *, nb):
    P = MOBA_BLOCK
    qb = pl.program_id(2)
    q = q_ref[0]
    scale = MOBA_HEAD_DIM ** -0.5

    km = km_ref[0].astype(BF16)
    km = jnp.concatenate([km, jnp.zeros((LANES - nb, MOBA_HEAD_DIM), BF16)], axis=0)
    gate = _dot_nt(q, km)
    lane = lax.broadcasted_iota(jnp.int32, (P, LANES), 1)
    cnt = jnp.zeros((P, LANES), F32)
    for m in range(nb - 1):
        gm = gate[:, m:m + 1]
        beats = jnp.where(gm > gate, 1.0, jnp.where(gm == gate, jnp.where(lane > m, 1.0, 0.0), 0.0))
        cnt = cnt + jnp.where(m < qb, beats, 0.0)
    sel_scr[...] = jnp.where(lane < qb, jnp.where(cnt < float(MOBA_TOPK), 1.0, 0.0), 0.0)

    r0 = pl.multiple_of(qb * P, P)
    k_own = k_ref[0, pl.ds(r0, P), :]
    v_own = v_ref[0, pl.ds(r0, P), :]
    s = _dot_nt(q, k_own) * scale
    row = lax.broadcasted_iota(jnp.int32, (P, P), 0)
    col = lax.broadcasted_iota(jnp.int32, (P, P), 1)
    s = jnp.where(col <= row, s, NEG_INF)
    m0 = jnp.max(s, axis=-1, keepdims=True)
    p = jnp.exp(s - m0)
    m_scr[...] = m0
    l_scr[...] = jnp.sum(p, axis=-1, keepdims=True)
    acc_scr[...] = _dot(p.astype(BF16), v_own)

    for j in range(nb - 1):
        @pl.when(j < qb)
        def _(j=j):
            kj = k_ref[0, j * P:(j + 1) * P, :]
            vj = v_ref[0, j * P:(j + 1) * P, :]
            sj = _dot_nt(q, kj) * scale
            sj = jnp.where(sel_scr[:, j:j + 1] > 0.0, sj, NEG_INF)
            m_old = m_scr[...]
            m_new = jnp.maximum(m_old, jnp.max(sj, axis=-1, keepdims=True))
            alpha = jnp.exp(m_old - m_new)
            pj = jnp.exp(sj - m_new)
            l_scr[...] = alpha * l_scr[...] + jnp.sum(pj, axis=-1, keepdims=True)
            acc_scr[...] = alpha * acc_scr[...] + _dot(pj.astype(BF16), vj)
            m_scr[...] = m_new

    o_ref[0] = (acc_scr[...] / l_scr[...]).astype(BF16)


def _moba(qn, kn, vb, kmean):
    B, S, W = qn.shape
    P = MOBA_BLOCK
    nb = S // P
    Dh = MOBA_HEAD_DIM
    kern = functools.partial(_moba_kernel, nb=nb)
    return pl.pallas_call(
        kern,
        out_shape=jax.ShapeDtypeStruct((B, S, W), BF16),
        grid=(B, MOBA_HEADS, nb),
        in_specs=[pl.BlockSpec((1, P, Dh), lambda b, h, n: (b, n, h)),
                  pl.BlockSpec((1, S, Dh), lambda b, h, n: (b, 0, h)),
                  pl.BlockSpec((1, S, Dh), lambda b, h, n: (b, 0, h)),
                  pl.BlockSpec((1, nb, Dh), lambda b, h, n: (b, 0, h))],
        out_specs=pl.BlockSpec((1, P, Dh), lambda b, h, n: (b, n, h)),
        scratch_shapes=[pltpu.VMEM((P, 1), F32), pltpu.VMEM((P, 1), F32),
                        pltpu.VMEM((P, Dh), F32), pltpu.VMEM((P, LANES), F32)],
        compiler_params=_params(("parallel", "parallel", "arbitrary")),
        name="moba",
    )(qn, kn, vb, kmean)


def _outproj_kernel(og_ref, om_ref, w_ref, x_ref, gate_ref, shift_ref, scale_ref, g_ref,
                    x1_ref, h2_ref):
    half = og_ref.shape[2]
    mix = _dot(og_ref[0], w_ref[0:half, :]) + _dot(om_ref[0], w_ref[half:2 * half, :])
    x1 = x_ref[0] + gate_ref[0] * mix
    x1_ref[0] = x1
    h2_ref[0] = _rms_mod(x1, g_ref[...], shift_ref[0], scale_ref[0]).astype(BF16)


def _outproj(o_gla, o_moba, w_out, x, mod3, norm_g):
    B, S, D = x.shape
    half = o_gla.shape[2]
    tm = 256
    return pl.pallas_call(
        _outproj_kernel,
        out_shape=(jax.ShapeDtypeStruct((B, S, D), F32),
                   jax.ShapeDtypeStruct((B, S, D), BF16)),
        grid=(B, S // tm),
        in_specs=[pl.BlockSpec((1, tm, half), lambda b, i: (b, i, 0)),
                  pl.BlockSpec((1, tm, half), lambda b, i: (b, i, 0)),
                  pl.BlockSpec((2 * half, D), lambda b, i: (0, 0)),
                  pl.BlockSpec((1, tm, D), lambda b, i: (b, i, 0)),
                  pl.BlockSpec((1, 1, D), lambda b, i: (b, 0, 2)),
                  pl.BlockSpec((1, 1, D), lambda b, i: (b, 0, 3)),
                  pl.BlockSpec((1, 1, D), lambda b, i: (b, 0, 4)),
                  pl.BlockSpec((1, D), lambda b, i: (0, 0))],
        out_specs=(pl.BlockSpec((1, tm, D), lambda b, i: (b, i, 0)),
                   pl.BlockSpec((1, tm, D), lambda b, i: (b, i, 0))),
        compiler_params=_params(("parallel", "parallel")),
        name="outproj",
    )(o_gla, o_moba, w_out, x, mod3, mod3, mod3, norm_g.reshape(1, D))


def _peer_scores_kernel(h_ref, wq_ref, sk_ref, s1_ref, s2_ref):
    q = _dot(h_ref[...], wq_ref[...]).astype(BF16)
    for h in range(PEER_HEADS):
        for c, o_ref in enumerate((s1_ref, s2_ref)):
            hc = 2 * h + c
            o_ref[h] = _dot_nt(sk_ref[hc], q[:, hc * LANES:(hc + 1) * LANES])


def _peer_scores(h2, w_q, sub_keys):
    T, D = h2.shape
    tm = 256
    out = jax.ShapeDtypeStruct((PEER_HEADS, PEER_N_KEYS, T), F32)
    ospec = pl.BlockSpec((PEER_HEADS, PEER_N_KEYS, tm), lambda i: (0, 0, i))
    return pl.pallas_call(
        _peer_scores_kernel,
        out_shape=(out, out),
        grid=(T // tm,),
        in_specs=[pl.BlockSpec((tm, D), lambda i: (i, 0)),
                  pl.BlockSpec(w_q.shape, lambda i: (0, 0)),
                  pl.BlockSpec(sub_keys.shape, lambda i: (0, 0, 0))],
        out_specs=(ospec, ospec),
        compiler_params=_params(("parallel",)),
        name="peer_scores",
    )(h2, w_q, sub_keys)


_STAIR_BLOCKS = (((0, 0),),
                 ((1, 0), (2, 8), (4, 13)),
                 ((3, 0), (5, 4), (6, 6), (7, 8), (8, 10), (9, 11), (10, 12), (11, 13), (12, 14), (13, 15)),
                 ((14, 0), (15, 1)))


def _top16(x, rows, rows16):
    srt = jnp.zeros((PEER_TOPK, x.shape[1]), F32)
    for r in range(PEER_TOPK):
        m = jnp.max(x, axis=0, keepdims=True)
        idx = jnp.min(jnp.where(x == m, rows, PEER_N_KEYS), axis=0, keepdims=True)
        x = jnp.where(rows == idx, NEG_INF, x)
        srt = jnp.where(rows16 == r, m, srt)
    return srt, x


def _peer_topk_kernel(s1_ref, s2_ref, e1_ref, e2_ref, tau_ref):
    tl = s1_ref.shape[2]
    K = PEER_TOPK
    rows = lax.broadcasted_iota(jnp.int32, (PEER_N_KEYS, tl), 0)
    rows16 = lax.broadcasted_iota(jnp.int32, (K, tl), 0)

    def head(h, carry):
        x1 = s1_ref[h]
        x2 = s2_ref[h]
        s1s, x1r = _top16(x1, rows, rows16)
        s2s, x2r = _top16(x2, rows, rows16)

        blocks = []
        for blk in _STAIR_BLOCKS:
            acc = jnp.full((K, tl), NEG_INF, F32)
            for p, off in blk:
                nr = K // (p + 1)
                shifted = s2s if off == 0 else pltpu.roll(s2s, off, 0)
                cand = shifted + s1s[p:p + 1, :]
                acc = jnp.where((rows16 >= off) & (rows16 < off + nr), cand, acc)
            blocks.append(acc)
        cand = jnp.concatenate(blocks, axis=0)
        n = jnp.zeros((1, tl), F32)
        tau = jnp.full((1, tl), NEG_INF, F32)
        for _ in range(K):
            m = jnp.max(cand, axis=0, keepdims=True)
            eq = cand == m
            tau = jnp.where(n < float(K), m, tau)
            n = n + jnp.sum(jnp.where(eq, 1.0, 0.0), axis=0, keepdims=True)
            cand = jnp.where(eq, NEG_INF, cand)

        e1s = jnp.exp(s1s - s1s[0:1, :])
        e2s = jnp.exp(s2s - s2s[0:1, :])
        z = jnp.zeros((1, tl), F32)
        for p in range(K):
            sel = jnp.where(s2s + s1s[p:p + 1, :] >= tau, e2s, 0.0)
            z = z + e1s[p:p + 1, :] * jnp.sum(sel, axis=0, keepdims=True)

        e1_ref[h] = jnp.where(x1r < x1, jnp.exp(x1 - s1s[0:1, :]), 0.0) / z
        e2_ref[h] = jnp.where(x2r < x2, jnp.exp(x2 - s2s[0:1, :]), 0.0)
        tau_ref[pl.ds(h, 1), :] = tau
        return carry

    lax.fori_loop(0, PEER_HEADS, head, 0)


def _peer_topk(s1T, s2T):
    H, nk, T = s1T.shape
    tl = LANES
    spec = pl.BlockSpec((H, nk, tl), lambda i: (0, 0, i))
    return pl.pallas_call(
        _peer_topk_kernel,
        out_shape=(jax.ShapeDtypeStruct((H, nk, T), F32),
                   jax.ShapeDtypeStruct((H, nk, T), F32),
                   jax.ShapeDtypeStruct((H, T), F32)),
        grid=(T // tl,),
        in_specs=[spec, spec],
        out_specs=(spec, spec, pl.BlockSpec((H, tl), lambda i: (0, i))),
        compiler_params=_params(("parallel",)),
        name="peer_topk",
    )(s1T, s2T)


def _gelu(x):
    return 0.5 * x * (1.0 + lax.erf(x * (1.0 / math.sqrt(2.0))))


def _peer_dense_kernel(h_ref, u_ref, vt_ref, s1_ref, e1_ref, s2_ref, e2_ref, tau_ref, x1_ref, g2_ref,
                       o_ref, acc_scr, a_scr, w_scr, *, tn, tm):
    e = pl.program_id(1)

    @pl.when(e == 0)
    def _():
        acc_scr[...] = jnp.zeros_like(acc_scr)

    a_scr[...] = _dot_nt(u_ref[...], h_ref[...])

    for lc in range(tm // LANES):
        ls = slice(lc * LANES, (lc + 1) * LANES)
        for ai in range(tn // PEER_N_KEYS):
            rs = slice(ai * PEER_N_KEYS, (ai + 1) * PEER_N_KEYS)
            gate = jnp.zeros((PEER_N_KEYS, LANES), F32)
            for h in range(PEER_HEADS):
                cs = s2_ref[h, :, ls] + s1_ref[h, ai:ai + 1, ls]
                sel = jnp.where(cs >= tau_ref[h:h + 1, ls], e2_ref[h, :, ls], 0.0)
                gate = gate + sel * e1_ref[h, ai:ai + 1, ls]
            w_scr[rs, ls] = (gate * _gelu(a_scr[rs, ls])).astype(BF16)

    acc_scr[...] += _dot(vt_ref[...], w_scr[...])

    @pl.when(e == pl.num_programs(1) - 1)
    def _():
        o_ref[...] = x1_ref[...] + g2_ref[0] * acc_scr[...].T


def _peer_dense(h2, u_bf, vt_bf, s1T, e1T, s2T, e2T, tau, x1, mod3):
    T, D = h2.shape
    E = u_bf.shape[0]
    tm, tn = 256, 1024
    H, nk, _ = s1T.shape
    na = tn // nk
    kern = functools.partial(_peer_dense_kernel, tn=tn, tm=tm)
    tiles_per_b = (T // tm) // mod3.shape[0]
    rows_spec = pl.BlockSpec((H, na, tm), lambda i, e: (0, e, i))
    full_spec = pl.BlockSpec((H, nk, tm), lambda i, e: (0, 0, i))
    return pl.pallas_call(
        kern,
        out_shape=jax.ShapeDtypeStruct((T, D), F32),
        grid=(T // tm, E // tn),
        in_specs=[pl.BlockSpec((tm, D), lambda i, e: (i, 0)),
                  pl.BlockSpec((tn, D), lambda i, e: (e, 0)),
                  pl.BlockSpec((D, tn), lambda i, e: (0, e)),
                  rows_spec, rows_spec, full_spec, full_spec,
                  pl.BlockSpec((H, tm), lambda i, e: (0, i)),
                  pl.BlockSpec((tm, D), lambda i, e: (i, 0)),
                  pl.BlockSpec((1, 1, D), lambda i, e: (i // tiles_per_b, 0, 5))],
        out_specs=pl.BlockSpec((tm, D), lambda i, e: (i, 0)),
        scratch_shapes=[pltpu.VMEM((D, tm), F32), pltpu.VMEM((tn, tm), F32), pltpu.VMEM((tn, tm), BF16)],
        compiler_params=_params(("parallel", "arbitrary")),
        name="peer_dense",
    )(h2, u_bf, vt_bf, s1T, e1T, s2T, e2T, tau, x1, mod3)


def kernel(x, c, positions, w_ada, b_ada, norm1_g, w_in, gla_w_a2, gla_b_a, gla_norm_g,
           moba_q_norm_g, moba_k_norm_g, w_out, norm2_g, peer_w_q, peer_sub_keys, peer_u, peer_v):
    B, S, D = x.shape
    T = B * S

    n_gla = 2 * GLA_HEADS * GLA_DK + 2 * GLA_HEADS * GLA_DV
    w_main = jnp.concatenate([w_in[:, :n_gla], w_in[:, n_gla + GLA_GATE_RANK:]], axis=1).astype(BF16)
    w_ga = jnp.pad(w_in[:, n_gla:n_gla + GLA_GATE_RANK], ((0, 0), (0, LANES - GLA_GATE_RANK))).astype(BF16)
    w_a2p = jnp.pad(gla_w_a2, ((0, LANES - GLA_GATE_RANK), (0, 0))).astype(BF16)
    half = ROPE_DIMS // 2
    inv_freq = jnp.power(jnp.float32(ROPE_THETA), -jnp.arange(half, dtype=F32) * (2.0 / ROPE_DIMS))
    invf = jnp.concatenate([inv_freq, inv_freq, jnp.zeros((LANES - ROPE_DIMS,), F32)]).reshape(1, LANES)
    pos3 = positions.astype(F32).reshape(B, S, 1)
    sub_keys = peer_sub_keys.reshape(-1, PEER_N_KEYS, peer_sub_keys.shape[-1]).astype(BF16)
    u_bf = peer_u.astype(BF16)
    vt_bf = peer_v.T.astype(BF16)

    mod3 = _ada(c, w_ada, b_ada).reshape(B, 1, 6 * D)
    proj, ga = _inproj(x, mod3, norm1_g, w_main, w_ga)
    o_gla = _gla(proj, ga, w_a2p, gla_b_a, gla_norm_g)
    qn, kn, vb, kmean = _moba_prep(proj, pos3, invf, moba_q_norm_g, moba_k_norm_g)
    o_moba = _moba(qn, kn, vb, kmean.reshape(B, S // MOBA_BLOCK, -1))
    x1, h2 = _outproj(o_gla, o_moba, w_out.astype(BF16), x, mod3, norm2_g)
    h2f = h2.reshape(T, D)
    s1T, s2T = _peer_scores(h2f, peer_w_q.astype(BF16), sub_keys)
    e1T, e2T, tau = _peer_topk(s1T, s2T)
    out = _peer_dense(h2f, u_bf, vt_bf, s1T, e1T, s2T, e2T, tau, x1.reshape(T, D), mod3)
    return out.reshape(B, S, D)
```

```python
import functools
import math

import jax
import jax.numpy as jnp
from jax import lax
from jax.experimental import pallas as pl
from jax.experimental.pallas import tpu as pltpu

F32 = jnp.float32
BF16 = jnp.bfloat16

EPS = 1e-6
GLA_HEADS = 8
GLA_DK = 64
GLA_DV = 128
GLA_GATE_RANK = 16
GLA_GATE_NORMALIZER = 16.0
GLA_CHUNK = 64
MOBA_HEADS = 8
MOBA_HEAD_DIM = 128
MOBA_BLOCK = 256
MOBA_TOPK = 3
ROPE_THETA = 500000.0
ROPE_DIMS = 32
PEER_HEADS = 8
PEER_N_KEYS = 128
PEER_TOPK = 16

LANES = 128
VMEM_LIMIT = 48 * 1024 * 1024

NEG_INF = float("-inf")


def _dot(a, b):
    return jnp.dot(a, b, preferred_element_type=F32)


def _dot_nt(a, b):
    return lax.dot_general(a, b, (((1,), (1,)), ((), ())), preferred_element_type=F32)


def _dot_tn(a, b):
    return lax.dot_general(a, b, (((0,), (0,)), ((), ())), preferred_element_type=F32)


def _params(sem):
    return pltpu.CompilerParams(dimension_semantics=sem, vmem_limit_bytes=VMEM_LIMIT)


def _ada_kernel(ct_ref, w_ref, b_ref, o_ref):
    ct = ct_ref[...]
    sc = ct * jax.nn.sigmoid(ct)
    w = w_ref[...]
    for b in range(ct.shape[1]):
        o_ref[b:b + 1, :] = jnp.sum(w * sc[:, b:b + 1], axis=0, keepdims=True) + b_ref[...]


def _ada(c, w_ada, b_ada):
    B, D = c.shape
    N = w_ada.shape[1]
    tn = 1024
    return pl.pallas_call(
        _ada_kernel,
        out_shape=jax.ShapeDtypeStruct((B, N), F32),
        grid=(N // tn,),
        in_specs=[pl.BlockSpec((D, B), lambda j: (0, 0)),
                  pl.BlockSpec((D, tn), lambda j: (0, j)),
                  pl.BlockSpec((1, tn), lambda j: (0, j))],
        out_specs=pl.BlockSpec((B, tn), lambda j: (0, j)),
        compiler_params=_params(("arbitrary",)),
        name="ada",
    )(c.T, w_ada, b_ada.reshape(1, N))


def _rms_mod(x, g, shift, scale):
    var = jnp.mean(x * x, axis=-1, keepdims=True)
    y = x * lax.rsqrt(var + EPS) * g
    return y * (1.0 + scale) + shift


def _inproj_kernel(x_ref, shift_ref, scale_ref, g_ref, w_ref, wga_ref, o_ref, ga_ref, h_scr):
    @pl.when(pl.program_id(2) == 0)
    def _():
        hb = _rms_mod(x_ref[0], g_ref[...], shift_ref[0], scale_ref[0]).astype(BF16)
        h_scr[...] = hb
        ga_ref[0] = _dot(hb, wga_ref[...])

    o_ref[0] = _dot(h_scr[...], w_ref[...])


def _inproj(x, mod3, norm_g, w_main, w_ga):
    B, S, D = x.shape
    N = w_main.shape[1]
    tm, tn = 512, 1024
    return pl.pallas_call(
        _inproj_kernel,
        out_shape=(jax.ShapeDtypeStruct((B, S, N), F32),
                   jax.ShapeDtypeStruct((B, S, LANES), F32)),
        grid=(B, S // tm, N // tn),
        in_specs=[pl.BlockSpec((1, tm, D), lambda b, i, j: (b, i, 0)),
                  pl.BlockSpec((1, 1, D), lambda b, i, j: (b, 0, 0)),
                  pl.BlockSpec((1, 1, D), lambda b, i, j: (b, 0, 1)),
                  pl.BlockSpec((1, D), lambda b, i, j: (0, 0)),
                  pl.BlockSpec((D, tn), lambda b, i, j: (0, j)),
                  pl.BlockSpec((D, LANES), lambda b, i, j: (0, 0))],
        out_specs=(pl.BlockSpec((1, tm, tn), lambda b, i, j: (b, i, j)),
                   pl.BlockSpec((1, tm, LANES), lambda b, i, j: (b, i, 0))),
        scratch_shapes=[pltpu.VMEM((tm, D), BF16)],
        compiler_params=_params(("parallel", "parallel", "arbitrary")),
        name="inproj",
    )(x, mod3, mod3, norm_g.reshape(1, D), w_main, w_ga)


def _log_sigmoid(z):
    return jnp.minimum(z, 0.0) - jnp.log1p(jnp.exp(-jnp.abs(z)))


def _gla_kernel(q_ref, k_ref, v_ref, gg_ref, ga_ref, wa_ref, ba_ref, gn_ref, o_ref, st_scr, *, ts):
    L = GLA_CHUNK

    @pl.when(pl.program_id(2) == 0)
    def _():
        st_scr[...] = jnp.zeros_like(st_scr)

    z = _dot(ga_ref[0].astype(BF16), wa_ref[...]) + ba_ref[...]
    la = _log_sigmoid(z) * (1.0 / GLA_GATE_NORMALIZER)

    row = lax.broadcasted_iota(jnp.int32, (L, L), 0)
    col = lax.broadcasted_iota(jnp.int32, (L, L), 1)
    causal = row >= col
    tri = jnp.where(causal, 1.0, 0.0).astype(BF16)
    lane = lax.broadcasted_iota(jnp.int32, (1, LANES), 1)
    head_mask = [lane < GLA_DK, lane >= GLA_DK]
    gn = gn_ref[...]

    states = [st_scr[0], st_scr[1]]
    for c in range(ts // L):
        r0 = c * L
        la_c = la[r0:r0 + L]
        hi = la_c.astype(BF16)
        lo = (la_c - hi.astype(F32)).astype(BF16)
        b = _dot(tri, hi) + _dot(tri, lo)
        b_last = b[L - 1:L, :]
        q_c = q_ref[0, r0:r0 + L, :]
        k_c = k_ref[0, r0:r0 + L, :]
        qd = q_c * (GLA_DK ** -0.5) * jnp.exp(b)
        kd = (k_c * jnp.exp(-b)).astype(BF16)
        ku = (k_c * jnp.exp(b_last - b)).astype(BF16)
        dec = jnp.exp(b_last)
        for h in range(2):
            qh = jnp.where(head_mask[h], qd, 0.0).astype(BF16)
            attn = jnp.where(causal, _dot_nt(qh, kd), 0.0)
            v_h = v_ref[0, r0:r0 + L, h * GLA_DV:(h + 1) * GLA_DV].astype(BF16)
            o = _dot(attn.astype(BF16), v_h) + _dot_nt(qh, states[h].astype(BF16))
            states[h] = states[h] * dec + _dot_tn(v_h, ku)
            var = jnp.mean(o * o, axis=-1, keepdims=True)
            on = o * lax.rsqrt(var + EPS) * gn
            g = gg_ref[0, r0:r0 + L, h * GLA_DV:(h + 1) * GLA_DV]
            o_ref[0, r0:r0 + L, h * GLA_DV:(h + 1) * GLA_DV] = (on * (g * jax.nn.sigmoid(g))).astype(BF16)
    st_scr[0] = states[0]
    st_scr[1] = states[1]


def _gla(proj, ga, w_a2p, b_a, gn):
    B, S, _ = proj.shape
    ts = 512
    hp = GLA_HEADS // 2
    kern = functools.partial(_gla_kernel, ts=ts)
    return pl.pallas_call(
        kern,
        out_shape=jax.ShapeDtypeStruct((B, S, GLA_HEADS * GLA_DV), BF16),
        grid=(B, hp, S // ts),
        in_specs=[pl.BlockSpec((1, ts, LANES), lambda b, p, s: (b, s, p)),
                  pl.BlockSpec((1, ts, LANES), lambda b, p, s: (b, s, 4 + p)),
                  pl.BlockSpec((1, ts, 2 * GLA_DV), lambda b, p, s: (b, s, 4 + p)),
                  pl.BlockSpec((1, ts, 2 * GLA_DV), lambda b, p, s: (b, s, 8 + p)),
                  pl.BlockSpec((1, ts, LANES), lambda b, p, s: (b, s, 0)),
                  pl.BlockSpec((LANES, LANES), lambda b, p, s: (0, p)),
                  pl.BlockSpec((1, LANES), lambda b, p, s: (0, p)),
                  pl.BlockSpec((1, GLA_DV), lambda b, p, s: (0, 0))],
        out_specs=pl.BlockSpec((1, ts, 2 * GLA_DV), lambda b, p, s: (b, s, p)),
        scratch_shapes=[pltpu.VMEM((2, GLA_DV, LANES), F32)],
        compiler_params=_params(("parallel", "parallel", "arbitrary")),
        name="gla",
    )(proj, proj, proj, proj, ga, w_a2p, b_a.reshape(1, -1), gn.reshape(1, -1))


def _moba_prep_kernel(q_ref, k_ref, v_ref, pos_ref, invf_ref, gq_ref, gk_ref,
                      qt_ref, ko_ref, vt_ref, km_ref):
    ang = pos_ref[0] * invf_ref[...]
    cs = jnp.cos(ang)
    sn = jnp.sin(ang)
    lane = lax.broadcasted_iota(jnp.int32, (1, LANES), 1)
    half = ROPE_DIMS // 2
    s_lo = jnp.where(lane < half, -sn, 0.0)
    s_hi = jnp.where(lane < half, 0.0, jnp.where(lane < ROPE_DIMS, sn, 0.0))

    def norm_rot(x, g):
        var = jnp.mean(x * x, axis=-1, keepdims=True)
        y = x * lax.rsqrt(var + EPS) * g
        return (y * cs + pltpu.roll(y, LANES - half, 1) * s_lo + pltpu.roll(y, half, 1) * s_hi)

    for h in range(MOBA_HEADS):
        sl = slice(h * MOBA_HEAD_DIM, (h + 1) * MOBA_HEAD_DIM)
        qr = norm_rot(q_ref[0, :, sl], gq_ref[...]) * (MOBA_HEAD_DIM ** -0.5)
        qt_ref[0, sl, :] = qr.T.astype(BF16)
        kr = norm_rot(k_ref[0, :, sl], gk_ref[...])
        ko_ref[0, :, sl] = kr.astype(BF16)
        km_ref[0, 0, :, sl] = jnp.mean(kr, axis=0, keepdims=True)
        vt_ref[0, sl, :] = v_ref[0, :, sl].T.astype(BF16)


def _moba_prep(proj, pos3, invf, gq, gk):
    B, S, _ = proj.shape
    P = MOBA_BLOCK
    W = MOBA_HEADS * MOBA_HEAD_DIM
    nb = S // P
    t_spec = pl.BlockSpec((1, W, P), lambda b, n: (b, 0, n))
    return pl.pallas_call(
        _moba_prep_kernel,
        out_shape=(jax.ShapeDtypeStruct((B, W, S), BF16),
                   jax.ShapeDtypeStruct((B, S, W), BF16),
                   jax.ShapeDtypeStruct((B, W, S), BF16),
                   jax.ShapeDtypeStruct((B, nb, 1, W), F32)),
        grid=(B, nb),
        in_specs=[pl.BlockSpec((1, P, W), lambda b, n: (b, n, 3)),
                  pl.BlockSpec((1, P, W), lambda b, n: (b, n, 4)),
                  pl.BlockSpec((1, P, W), lambda b, n: (b, n, 5)),
                  pl.BlockSpec((1, P, 1), lambda b, n: (b, n, 0)),
                  pl.BlockSpec((1, LANES), lambda b, n: (0, 0)),
                  pl.BlockSpec((1, MOBA_HEAD_DIM), lambda b, n: (0, 0)),
                  pl.BlockSpec((1, MOBA_HEAD_DIM), lambda b, n: (0, 0))],
        out_specs=(t_spec,
                   pl.BlockSpec((1, P, W), lambda b, n: (b, n, 0)),
                   t_spec,
                   pl.BlockSpec((1, 1, 1, W), lambda b, n: (b, n, 0, 0))),
        compiler_params=_params(("parallel", "parallel")),
        name="moba_prep",
    )(proj, proj, proj, pos3, invf, gq.reshape(1, -1), gk.reshape(1, -1))


MOBA_MASK_BIAS = -1e30


def _moba_kernel(qt_ref, k_ref, vt_ref, km_ref, o_ref, qa_scr, m_scr, l_scr, acc_scr, *, nb, hg):
    P = MOBA_BLOCK
    Dh = MOBA_HEAD_DIM
    qb = pl.program_id(2)
    r_own = pl.multiple_of(qb * P, P)
    blk = lax.broadcasted_iota(jnp.int32, (nb, P), 0)
    key = lax.broadcasted_iota(jnp.int32, (P, P), 0)
    qry = lax.broadcasted_iota(jnp.int32, (P, P), 1)

    heads = [slice(h * Dh, (h + 1) * Dh) for h in range(hg)]
    qts = [qt_ref[0, hs, :] for hs in heads]

    own = [_dot(k_ref[0, pl.ds(r_own, P), hs], qt) for hs, qt in zip(heads, qts)]
    gates = [_dot(km_ref[0, :, hs].astype(BF16), qt) for hs, qt in zip(heads, qts)]
    ps = []
    for h in range(hg):
        s = jnp.where(key <= qry, own[h], NEG_INF)
        m0 = jnp.max(s, axis=0, keepdims=True)
        p = jnp.exp(s - m0)
        m_scr[h] = m0
        l_scr[h] = jnp.sum(p, axis=0, keepdims=True)
        ps.append(p.astype(BF16))
    for h in range(hg):
        acc_scr[h] = _dot(vt_ref[0, heads[h], pl.ds(r_own, P)], ps[h])

    for h in range(hg):
        g = gates[h]
        cnt = jnp.zeros((nb, P), F32)
        for m in range(nb - 1):
            gm = g[m:m + 1, :]
            beats = jnp.where(gm > g, 1.0, jnp.where(gm == g, jnp.where(blk > m, 1.0, 0.0), 0.0))
            cnt = cnt + jnp.where(m < qb, beats, 0.0)
        bias = jnp.where(blk < qb, jnp.where(cnt < float(MOBA_TOPK), 0.0, MOBA_MASK_BIAS), MOBA_MASK_BIAS)
        qa_scr[h, 0:Dh, :] = qts[h]
        qa_scr[h, Dh:2 * Dh, :] = jnp.concatenate([bias, jnp.zeros((Dh - nb, P), F32)], axis=0).astype(BF16)

    lane = lax.broadcasted_iota(jnp.int32, (2 * P, LANES), 1)
    second = jnp.where(lax.broadcasted_iota(jnp.int32, (2 * P, LANES), 0) >= P, 1, 0)

    def pair(i, carry):
        j0 = 2 * i
        r0 = pl.multiple_of(j0 * P, 2 * P)
        onehot = jnp.where(lane == j0 + second, 1.0, 0.0).astype(BF16)
        sjs = []
        for h in range(hg):
            hs = slice(h * Dh, (h + 1) * Dh)
            k_aug = jnp.concatenate([k_ref[0, pl.ds(r0, 2 * P), hs], onehot], axis=1)
            sjs.append(_dot(k_aug, qa_scr[h]))
        pjs = []
        alphas = []
        for h in range(hg):
            m_old = m_scr[h]
            m_new = jnp.maximum(m_old, jnp.max(sjs[h], axis=0, keepdims=True))
            alpha = jnp.exp(m_old - m_new)
            pj = jnp.exp(sjs[h] - m_new)
            l_scr[h] = alpha * l_scr[h] + jnp.sum(pj, axis=0, keepdims=True)
            m_scr[h] = m_new
            pjs.append(pj.astype(BF16))
            alphas.append(alpha)
        for h in range(hg):
            hs = slice(h * Dh, (h + 1) * Dh)
            acc_scr[h] = alphas[h] * acc_scr[h] + _dot(vt_ref[0, hs, pl.ds(r0, 2 * P)], pjs[h])
        return carry

    lax.fori_loop(0, (qb + 1) // 2, pair, 0)

    for h in range(hg):
        o_ref[0, :, h * Dh:(h + 1) * Dh] = (acc_scr[h] / l_scr[h]).T.astype(BF16)


def _moba(qT, kn, vT, kmean):
    B, S, W = kn.shape
    P = MOBA_BLOCK
    nb = S // P
    Dh = MOBA_HEAD_DIM
    hg = 4
    kern = functools.partial(_moba_kernel, nb=nb, hg=hg)
    return pl.pallas_call(
        kern,
        out_shape=jax.ShapeDtypeStruct((B, S, W), BF16),
        grid=(B, MOBA_HEADS // hg, nb),
        in_specs=[pl.BlockSpec((1, hg * Dh, P), lambda b, h, n: (b, h, n)),
                  pl.BlockSpec((1, S, hg * Dh), lambda b, h, n: (b, 0, h)),
                  pl.BlockSpec((1, hg * Dh, S), lambda b, h, n: (b, h, 0)),
                  pl.BlockSpec((1, nb, hg * Dh), lambda b, h, n: (b, 0, h))],
        out_specs=pl.BlockSpec((1, P, hg * Dh), lambda b, h, n: (b, n, h)),
        scratch_shapes=[pltpu.VMEM((hg, 2 * Dh, P), BF16), pltpu.VMEM((hg, 1, P), F32),
                        pltpu.VMEM((hg, 1, P), F32), pltpu.VMEM((hg, Dh, P), F32)],
        compiler_params=_params(("parallel", "parallel", "arbitrary")),
        name="moba",
    )(qT, kn, vT, kmean)


def _outproj_kernel(og_ref, om_ref, w_ref, x_ref, gate_ref, shift_ref, scale_ref, g_ref,
                    x1_ref, h2_ref):
    half = og_ref.shape[2]
    mix = _dot(og_ref[0], w_ref[0:half, :]) + _dot(om_ref[0], w_ref[half:2 * half, :])
    x1 = x_ref[0] + gate_ref[0] * mix
    x1_ref[0] = x1
    h2_ref[0] = _rms_mod(x1, g_ref[...], shift_ref[0], scale_ref[0]).astype(BF16)


def _outproj(o_gla, o_moba, w_out, x, mod3, norm_g):
    B, S, D = x.shape
    half = o_gla.shape[2]
    tm = 256
    return pl.pallas_call(
        _outproj_kernel,
        out_shape=(jax.ShapeDtypeStruct((B, S, D), F32),
                   jax.ShapeDtypeStruct((B, S, D), BF16)),
        grid=(B, S // tm),
        in_specs=[pl.BlockSpec((1, tm, half), lambda b, i: (b, i, 0)),
                  pl.BlockSpec((1, tm, half), lambda b, i: (b, i, 0)),
                  pl.BlockSpec((2 * half, D), lambda b, i: (0, 0)),
                  pl.BlockSpec((1, tm, D), lambda b, i: (b, i, 0)),
                  pl.BlockSpec((1, 1, D), lambda b, i: (b, 0, 2)),
                  pl.BlockSpec((1, 1, D), lambda b, i: (b, 0, 3)),
                  pl.BlockSpec((1, 1, D), lambda b, i: (b, 0, 4)),
                  pl.BlockSpec((1, D), lambda b, i: (0, 0))],
        out_specs=(pl.BlockSpec((1, tm, D), lambda b, i: (b, i, 0)),
                   pl.BlockSpec((1, tm, D), lambda b, i: (b, i, 0))),
        compiler_params=_params(("parallel", "parallel")),
        name="outproj",
    )(o_gla, o_moba, w_out, x, mod3, mod3, mod3, norm_g.reshape(1, D))


def _peer_scores_kernel(h_ref, wq_ref, sk_ref, s1_ref, s2_ref):
    q = _dot(h_ref[...], wq_ref[...]).astype(BF16)
    for h in range(PEER_HEADS):
        for c, o_ref in enumerate((s1_ref, s2_ref)):
            hc = 2 * h + c
            o_ref[h] = _dot_nt(sk_ref[hc], q[:, hc * LANES:(hc + 1) * LANES])


def _peer_scores(h2, w_q, sub_keys):
    T, D = h2.shape
    tm = 256
    out = jax.ShapeDtypeStruct((PEER_HEADS, PEER_N_KEYS, T), F32)
    ospec = pl.BlockSpec((PEER_HEADS, PEER_N_KEYS, tm), lambda i: (0, 0, i))
    return pl.pallas_call(
        _peer_scores_kernel,
        out_shape=(out, out),
        grid=(T // tm,),
        in_specs=[pl.BlockSpec((tm, D), lambda i: (i, 0)),
                  pl.BlockSpec(w_q.shape, lambda i: (0, 0)),
                  pl.BlockSpec(sub_keys.shape, lambda i: (0, 0, 0))],
        out_specs=(ospec, ospec),
        compiler_params=_params(("parallel",)),
        name="peer_scores",
    )(h2, w_q, sub_keys)


_STAIR_BLOCKS = (((0, 0),),
                 ((1, 0), (2, 8), (4, 13)),
                 ((3, 0), (5, 4), (6, 6), (7, 8), (8, 10), (9, 11), (10, 12), (11, 13), (12, 14), (13, 15)),
                 ((14, 0), (15, 1)))


def _top16(x, rows, rows16):
    srt = jnp.zeros((PEER_TOPK, x.shape[1]), F32)
    for r in range(PEER_TOPK):
        m = jnp.max(x, axis=0, keepdims=True)
        idx = jnp.min(jnp.where(x == m, rows, PEER_N_KEYS), axis=0, keepdims=True)
        x = jnp.where(rows == idx, NEG_INF, x)
        srt = jnp.where(rows16 == r, m, srt)
    return srt, x


def _peer_topk_kernel(s1_ref, s2_ref, e1_ref, e2_ref, tau_ref):
    tl = s1_ref.shape[2]
    K = PEER_TOPK
    rows = lax.broadcasted_iota(jnp.int32, (PEER_N_KEYS, tl), 0)
    rows16 = lax.broadcasted_iota(jnp.int32, (K, tl), 0)

    def head(h, carry):
        x1 = s1_ref[h]
        x2 = s2_ref[h]
        s1s, x1r = _top16(x1, rows, rows16)
        s2s, x2r = _top16(x2, rows, rows16)

        blocks = []
        for blk in _STAIR_BLOCKS:
            acc = jnp.full((K, tl), NEG_INF, F32)
            for p, off in blk:
                nr = K // (p + 1)
                shifted = s2s if off == 0 else pltpu.roll(s2s, off, 0)
                cand = shifted + s1s[p:p + 1, :]
                acc = jnp.where((rows16 >= off) & (rows16 < off + nr), cand, acc)
            blocks.append(acc)
        cand = jnp.concatenate(blocks, axis=0)
        n = jnp.zeros((1, tl), F32)
        tau = jnp.full((1, tl), NEG_INF, F32)
        for _ in range(K):
            m = jnp.max(cand, axis=0, keepdims=True)
            eq = cand == m
            tau = jnp.where(n < float(K), m, tau)
            n = n + jnp.sum(jnp.where(eq, 1.0, 0.0), axis=0, keepdims=True)
            cand = jnp.where(eq, NEG_INF, cand)

        e1s = jnp.exp(s1s - s1s[0:1, :])
        e2s = jnp.exp(s2s - s2s[0:1, :])
        z = jnp.zeros((1, tl), F32)
        for p in range(K):
            sel = jnp.where(s2s + s1s[p:p + 1, :] >= tau, e2s, 0.0)
            z = z + e1s[p:p + 1, :] * jnp.sum(sel, axis=0, keepdims=True)

        e1_ref[h] = jnp.where(x1r < x1, jnp.exp(x1 - s1s[0:1, :]), 0.0) / z
        e2_ref[h] = jnp.where(x2r < x2, jnp.exp(x2 - s2s[0:1, :]), 0.0)
        tau_ref[pl.ds(h, 1), :] = tau
        return carry

    lax.fori_loop(0, PEER_HEADS, head, 0)


def _peer_topk(s1T, s2T):
    H, nk, T = s1T.shape
    tl = LANES
    spec = pl.BlockSpec((H, nk, tl), lambda i: (0, 0, i))
    return pl.pallas_call(
        _peer_topk_kernel,
        out_shape=(jax.ShapeDtypeStruct((H, nk, T), F32),
                   jax.ShapeDtypeStruct((H, nk, T), F32),
                   jax.ShapeDtypeStruct((H, T), F32)),
        grid=(T // tl,),
        in_specs=[spec, spec],
        out_specs=(spec, spec, pl.BlockSpec((H, tl), lambda i: (0, i))),
        compiler_params=_params(("parallel",)),
        name="peer_topk",
    )(s1T, s2T)


def _gelu(x):
    return 0.5 * x * (1.0 + lax.erf(x * (1.0 / math.sqrt(2.0))))


def _peer_dense_kernel(h_ref, u_ref, vt_ref, s1_ref, e1_ref, s2_ref, e2_ref, tau_ref, x1_ref, g2_ref,
                       o_ref, acc_scr, a_scr, w_scr, *, tn, tm):
    e = pl.program_id(1)

    @pl.when(e == 0)
    def _():
        acc_scr[...] = jnp.zeros_like(acc_scr)

    a_scr[...] = _dot_nt(u_ref[...], h_ref[...])

    for lc in range(tm // LANES):
        ls = slice(lc * LANES, (lc + 1) * LANES)
        for ai in range(tn // PEER_N_KEYS):
            rs = slice(ai * PEER_N_KEYS, (ai + 1) * PEER_N_KEYS)
            gate = jnp.zeros((PEER_N_KEYS, LANES), F32)
            for h in range(PEER_HEADS):
                cs = s2_ref[h, :, ls] + s1_ref[h, ai:ai + 1, ls]
                sel = jnp.where(cs >= tau_ref[h:h + 1, ls], e2_ref[h, :, ls], 0.0)
                gate = gate + sel * e1_ref[h, ai:ai + 1, ls]
            w_scr[rs, ls] = (gate * _gelu(a_scr[rs, ls])).astype(BF16)

    acc_scr[...] += _dot(vt_ref[...], w_scr[...])

    @pl.when(e == pl.num_programs(1) - 1)
    def _():
        o_ref[...] = x1_ref[...] + g2_ref[0] * acc_scr[...].T


def _peer_dense(h2, u_bf, vt_bf, s1T, e1T, s2T, e2T, tau, x1, mod3):
    T, D = h2.shape
    E = u_bf.shape[0]
    tm, tn = 256, 1024
    H, nk, _ = s1T.shape
    na = tn // nk
    kern = functools.partial(_peer_dense_kernel, tn=tn, tm=tm)
    tiles_per_b = (T // tm) // mod3.shape[0]
    rows_spec = pl.BlockSpec((H, na, tm), lambda i, e: (0, e, i))
    full_spec = pl.BlockSpec((H, nk, tm), lambda i, e: (0, 0, i))
    return pl.pallas_call(
        kern,
        out_shape=jax.ShapeDtypeStruct((T, D), F32),
        grid=(T // tm, E // tn),
        in_specs=[pl.BlockSpec((tm, D), lambda i, e: (i, 0)),
                  pl.BlockSpec((tn, D), lambda i, e: (e, 0)),
                  pl.BlockSpec((D, tn), lambda i, e: (0, e)),
                  rows_spec, rows_spec, full_spec, full_spec,
                  pl.BlockSpec((H, tm), lambda i, e: (0, i)),
                  pl.BlockSpec((tm, D), lambda i, e: (i, 0)),
                  pl.BlockSpec((1, 1, D), lambda i, e: (i // tiles_per_b, 0, 5))],
        out_specs=pl.BlockSpec((tm, D), lambda i, e: (i, 0)),
        scratch_shapes=[pltpu.VMEM((D, tm), F32), pltpu.VMEM((tn, tm), F32), pltpu.VMEM((tn, tm), BF16)],
        compiler_params=_params(("parallel", "arbitrary")),
        name="peer_dense",
    )(h2, u_bf, vt_bf, s1T, e1T, s2T, e2T, tau, x1, mod3)


def kernel(x, c, positions, w_ada, b_ada, norm1_g, w_in, gla_w_a2, gla_b_a, gla_norm_g,
           moba_q_norm_g, moba_k_norm_g, w_out, norm2_g, peer_w_q, peer_sub_keys, peer_u, peer_v):
    B, S, D = x.shape
    T = B * S

    n_gla = 2 * GLA_HEADS * GLA_DK + 2 * GLA_HEADS * GLA_DV
    w_main = jnp.concatenate([w_in[:, :n_gla], w_in[:, n_gla + GLA_GATE_RANK:]], axis=1).astype(BF16)
    w_ga = jnp.pad(w_in[:, n_gla:n_gla + GLA_GATE_RANK], ((0, 0), (0, LANES - GLA_GATE_RANK))).astype(BF16)
    w_a2p = jnp.pad(gla_w_a2, ((0, LANES - GLA_GATE_RANK), (0, 0))).astype(BF16)
    half = ROPE_DIMS // 2
    inv_freq = jnp.power(jnp.float32(ROPE_THETA), -jnp.arange(half, dtype=F32) * (2.0 / ROPE_DIMS))
    invf = jnp.concatenate([inv_freq, inv_freq, jnp.zeros((LANES - ROPE_DIMS,), F32)]).reshape(1, LANES)
    pos3 = positions.astype(F32).reshape(B, S, 1)
    sub_keys = peer_sub_keys.reshape(-1, PEER_N_KEYS, peer_sub_keys.shape[-1]).astype(BF16)
    u_bf = peer_u.astype(BF16)
    vt_bf = peer_v.T.astype(BF16)

    mod3 = _ada(c, w_ada, b_ada).reshape(B, 1, 6 * D)
    proj, ga = _inproj(x, mod3, norm1_g, w_main, w_ga)
    o_gla = _gla(proj, ga, w_a2p, gla_b_a, gla_norm_g)
    qT, kn, vT, kmean = _moba_prep(proj, pos3, invf, moba_q_norm_g, moba_k_norm_g)
    o_moba = _moba(qT, kn, vT, kmean.reshape(B, S // MOBA_BLOCK, -1))
    x1, h2 = _outproj(o_gla, o_moba, w_out.astype(BF16), x, mod3, norm2_g)
    h2f = h2.reshape(T, D)
    s1T, s2T = _peer_scores(h2f, peer_w_q.astype(BF16), sub_keys)
    e1T, e2T, tau = _peer_topk(s1T, s2T)
    out = _peer_dense(h2f, u_bf, vt_bf, s1T, e1T, s2T, e2T, tau, x1.reshape(T, D), mod3)
    return out.reshape(B, S, D)
```
